```python
import math
import jax
import jax.numpy as jnp
from jax import lax
import numpy as np

D_MODEL = 2048
BATCH = 4
SEQ = 4096
DEPTH = 4

GRID_W = 64
CTX_LEN = 256
N_MIXERS = 4
HEAD_DIM = 128
Q_BLOCK = 128
ROPE_THETA = 10000.0
EPS = 1e-6
NEG_INF = -1e30
FFN_HIDDEN = -(-8 * D_MODEL // (3 * 256)) * 256

SWA_Q_HEADS = D_MODEL // HEAD_DIM
SWA_KV_HEADS = SWA_Q_HEADS // 4
SWA_WINDOW = 128
DIFF_HEADS = D_MODEL // (2 * HEAD_DIM)
MLA_HEADS = D_MODEL // HEAD_DIM
MLA_Q_RANK = D_MODEL // 4
MLA_KV_RANK = 512
MLA_NOPE_DIM = 128
MLA_ROPE_DIM = 64
MLA_V_DIM = 128
NA_HEADS = D_MODEL // HEAD_DIM
NA_WIN_ROWS = 8
NA_WIN_COLS = 16
NA_QCOLS = 16
NA_KCOLS = NA_QCOLS + NA_WIN_COLS

kernel_name = 'hybrid_interleaved_diffusion_block'


def rmsnorm(x, g):
    xf = x.astype(jnp.float32)
    y = xf * lax.rsqrt(jnp.mean(xf * xf, axis=-1, keepdims=True) + EPS)
    return (y * g.astype(jnp.float32)).astype(x.dtype)


def adaln(cond, w, b):
    return jnp.split(jax.nn.silu(cond) @ w + b, 6, axis=-1)


def axial_angles(n_tokens, rot_dim):
    n = rot_dim // 4
    t = jnp.arange(n_tokens)
    row = (t // GRID_W).astype(jnp.float32)
    col = (t % GRID_W).astype(jnp.float32)
    inv = ROPE_THETA ** (-jnp.arange(n, dtype=jnp.float32) / n)
    return row[:, None] * inv, col[:, None] * inv


def _rotate(x, ang):
    n = ang.shape[-1]
    shape = (ang.shape[0],) + (1,) * (x.ndim - 3) + (n,)
    cos = jnp.cos(ang).reshape(shape).astype(x.dtype)
    sin = jnp.sin(ang).reshape(shape).astype(x.dtype)
    x1, x2 = x[..., :n], x[..., n:]
    return jnp.concatenate([x1 * cos - x2 * sin, x1 * sin + x2 * cos], axis=-1)


def axial_rope(x, ang):
    ang_r, ang_c = ang
    h = x.shape[-1] // 2
    return jnp.concatenate([_rotate(x[..., :h], ang_r), _rotate(x[..., h:], ang_c)], axis=-1)


def sweep_query_blocks(fn, qs):
    b, s = qs[0].shape[:2]
    nb = s // Q_BLOCK
    blocks = tuple(jnp.moveaxis(q.reshape(b, nb, Q_BLOCK, *q.shape[2:]), 1, 0) for q in qs)
    o = jnp.moveaxis(lax.map(fn, blocks), 0, 1)
    return o.reshape(b, s, *o.shape[3:])


def swiglu(u, w_gu, w_down):
    g, up = jnp.split(u @ w_gu, 2, axis=-1)
    return (jax.nn.silu(g) * up) @ w_down


def swa_mixer(ul, uc, wqkv, wo, sink, ang, need_ctx):
    hq, hk, dh = SWA_Q_HEADS, SWA_KV_HEADS, HEAD_DIM
    g = hq // hk
    scale = dh ** -0.5

    def proj(u):
        b, n, _ = u.shape
        q, k, v = jnp.split(u @ wqkv, [hq * dh, (hq + hk) * dh], axis=-1)
        return q.reshape(b, n, hk, g, dh) * scale, k.reshape(b, n, hk, dh), v.reshape(b, n, hk, dh)

    ql, kl, vl = proj(ul)
    qc, kc, vc = proj(uc)
    ql = axial_rope(ql, ang)
    kl = axial_rope(kl, ang)
    b, s = ul.shape[:2]
    nb, qb_len = s // Q_BLOCK, Q_BLOCK
    sink_hg = sink.astype(jnp.float32).reshape(hk, g)

    qb = ql.reshape(b, nb, qb_len, hk, g, dh)

    def band(t):
        tp = jnp.pad(t, ((0, 0), (qb_len, qb_len), (0, 0), (0, 0))).reshape(b, nb + 2, qb_len, hk, dh)
        return jnp.concatenate([tp[:, :-2], tp[:, 1:-1], tp[:, 2:]], axis=2)

    kb, vb = band(kl), band(vl)
    qpos = jnp.arange(qb_len)
    kpos = jnp.arange(3 * qb_len) - qb_len
    in_win = jnp.abs(kpos[None, :] - qpos[:, None]) <= SWA_WINDOW
    kabs = jnp.arange(nb)[:, None] * qb_len + kpos[None, :]
    in_seq = (kabs >= 0) & (kabs < s)
    mask = in_win[None] & in_seq[:, None, :]
    s_band = jnp.einsum('bnqhgd,bnkhd->bnhgqk', qb, kb).astype(jnp.float32)
    s_band = jnp.where(mask[None, :, None, None], s_band, NEG_INF)
    s_ctx = jnp.einsum('bnqhgd,bchd->bnhgqc', qb, kc).astype(jnp.float32)
    sink_col = jnp.broadcast_to(sink_hg[:, :, None, None], (b, nb, hk, g, qb_len, 1))
    p = jax.nn.softmax(jnp.concatenate([s_band, s_ctx, sink_col], axis=-1), axis=-1).astype(vl.dtype)
    nband = 3 * qb_len
    o = (jnp.einsum('bnhgqk,bnkhd->bnqhgd', p[..., :nband], vb)
         + jnp.einsum('bnhgqc,bchd->bnqhgd', p[..., nband:-1], vc))
    yl = o.reshape(b, s, hq * dh) @ wo

    yc = None
    if need_ctx:
        nc = uc.shape[1]
        s_cc = jnp.einsum('bqhgd,bkhd->bhgqk', qc, kc).astype(jnp.float32)
        sink_c = jnp.broadcast_to(sink_hg[:, :, None, None], (b, hk, g, nc, 1))
        pc = jax.nn.softmax(jnp.concatenate([s_cc, sink_c], axis=-1), axis=-1).astype(vc.dtype)
        oc = jnp.einsum('bhgqk,bkhd->bqhgd', pc[..., :-1], vc)
        yc = oc.reshape(b, nc, hq * dh) @ wo
    return yl, yc


def diff_mixer(ul, uc, wqkv, wo, lam_vecs, subln, lam_init, ang, need_ctx):
    h, dh = DIFF_HEADS, HEAD_DIM
    scale = dh ** -0.5

    def proj(u):
        b, n, _ = u.shape
        q, k, v = jnp.split(u @ wqkv, 3, axis=-1)
        return q.reshape(b, n, h, 2, dh) * scale, k.reshape(b, n, h, 2, dh), v.reshape(b, n, h, 2 * dh)

    ql, kl, vl = proj(ul)
    qc, kc, vc = proj(uc)
    ql = axial_rope(ql, ang)
    kl = axial_rope(kl, ang)
    lv = lam_vecs.astype(jnp.float32)
    lam = jnp.exp(jnp.sum(lv[0] * lv[1])) - jnp.exp(jnp.sum(lv[2] * lv[3])) + lam_init

    def attend(q, k, v):
        sc = jnp.einsum('bqhcd,bkhcd->bhcqk', q, k).astype(jnp.float32)
        p = jax.nn.softmax(sc, axis=-1)
        a = (p[:, :, 0] - lam * p[:, :, 1]).astype(v.dtype)
        return jnp.einsum('bhqk,bkhe->bqhe', a, v)

    def finish(o):
        o = rmsnorm(o, subln) * (1.0 - lam_init)
        return o.reshape(o.shape[0], o.shape[1], h * 2 * dh) @ wo

    k_all = jnp.concatenate([kc, kl], axis=1)
    v_all = jnp.concatenate([vc, vl], axis=1)
    yl = finish(sweep_query_blocks(lambda qs: attend(qs[0], k_all, v_all), (ql,)))
    yc = finish(attend(qc, kc, vc)) if need_ctx else None
    return yl, yc


def mla_mixer(ul, uc, wdown, q_norm, kv_norm, wuq, wukv, wo, ang, need_ctx):
    h, dn, dr, dv = MLA_HEADS, MLA_NOPE_DIM, MLA_ROPE_DIM, MLA_V_DIM
    scale = (dn + dr) ** -0.5

    def proj(u):
        b, n, _ = u.shape
        cq, ckv, kpe = jnp.split(u @ wdown, [MLA_Q_RANK, MLA_Q_RANK + MLA_KV_RANK], axis=-1)
        q = (rmsnorm(cq, q_norm) @ wuq).reshape(b, n, h, dn + dr) * scale
        kv = (rmsnorm(ckv, kv_norm) @ wukv).reshape(b, n, h, dn + dv)
        return q[..., :dn], q[..., dn:], kv[..., :dn], kpe, kv[..., dn:]

    qn_l, qp_l, kn_l, kp_l, v_l = proj(ul)
    qn_c, qp_c, kn_c, kp_c, v_c = proj(uc)
    qp_l = axial_rope(qp_l, ang)
    kp_l = axial_rope(kp_l, ang)

    def attend(qn, qp, kn, kp, v):
        sc = (jnp.einsum('bqhd,bkhd->bhqk', qn, kn)
              + jnp.einsum('bqhr,bkr->bhqk', qp, kp)).astype(jnp.float32)
        p = jax.nn.softmax(sc, axis=-1).astype(v.dtype)
        return jnp.einsum('bhqk,bkhd->bqhd', p, v)

    kn_all = jnp.concatenate([kn_c, kn_l], axis=1)
    kp_all = jnp.concatenate([kp_c, kp_l], axis=1)
    v_all = jnp.concatenate([v_c, v_l], axis=1)
    b, s = ul.shape[:2]
    ol = sweep_query_blocks(lambda qs: attend(qs[0], qs[1], kn_all, kp_all, v_all), (qn_l, qp_l))
    yl = ol.reshape(b, s, h * dv) @ wo
    yc = None
    if need_ctx:
        oc = attend(qn_c, qp_c, kn_c, kp_c, v_c)
        yc = oc.reshape(b, uc.shape[1], h * dv) @ wo
    return yl, yc


def na_mixer(ul, uc, wqkv, wo, rpb, need_ctx):
    h, dh = NA_HEADS, HEAD_DIM
    scale = dh ** -0.5

    def proj(u):
        b, n, _ = u.shape
        q, k, v = jnp.split(u @ wqkv, 3, axis=-1)
        return q.reshape(b, n, h, dh) * scale, k.reshape(b, n, h, dh), v.reshape(b, n, h, dh)

    ql, kl, vl = proj(ul)
    qc, kc, vc = proj(uc)
    b, s = ul.shape[:2]
    rows = s // GRID_W
    wr = min(NA_WIN_ROWS, rows)
    ncb = GRID_W // NA_QCOLS
    starts = [min(max(j * NA_QCOLS - NA_WIN_COLS // 2, 0), GRID_W - NA_KCOLS) for j in range(ncb)]
    qcol = np.arange(GRID_W).reshape(ncb, NA_QCOLS)
    c0 = np.clip(qcol - NA_WIN_COLS // 2, 0, GRID_W - NA_WIN_COLS)
    kcol = np.array(starts)[:, None] + np.arange(NA_KCOLS)[None, :]
    col_ok = (kcol[:, None, :] >= c0[:, :, None]) & (kcol[:, None, :] < c0[:, :, None] + NA_WIN_COLS)
    col_idx = np.clip(kcol[:, None, :] - qcol[:, :, None] + NA_WIN_COLS - 1, 0, 2 * NA_WIN_COLS - 2)
    nk = wr * NA_KCOLS
    mask = np.broadcast_to(col_ok[:, :, None, :], (ncb, NA_QCOLS, wr, NA_KCOLS)).reshape(ncb, NA_QCOLS, nk)
    kg = kl.reshape(b, rows, GRID_W, h, dh)
    vg = vl.reshape(b, rows, GRID_W, h, dh)
    rpb32 = rpb.astype(jnp.float32)

    def row_fn(args):
        r, qr = args
        r0 = jnp.clip(r - wr // 2, 0, rows - wr)

        def gather(t):
            tr = lax.dynamic_slice_in_dim(t, r0, wr, axis=1)
            tb = jnp.stack([tr[:, :, st:st + NA_KCOLS] for st in starts], axis=1)
            return tb.reshape(b, ncb, nk, h, dh)

        kb, vb = gather(kg), gather(vg)
        qb = qr.reshape(b, ncb, NA_QCOLS, h, dh)
        row_off = r0 + jnp.arange(wr) - r + NA_WIN_ROWS - 1
        bias = rpb32[:, row_off][:, :, col_idx]
        bias = bias.transpose(0, 2, 3, 1, 4).reshape(h, ncb, NA_QCOLS, nk)
        s_nb = jnp.einsum('bjqhd,bjkhd->bhjqk', qb, kb).astype(jnp.float32) + bias
        s_nb = jnp.where(mask, s_nb, NEG_INF)
        s_cx = jnp.einsum('bjqhd,bchd->bhjqc', qb, kc).astype(jnp.float32)
        p = jax.nn.softmax(jnp.concatenate([s_nb, s_cx], axis=-1), axis=-1).astype(vb.dtype)
        o = (jnp.einsum('bhjqk,bjkhd->bjqhd', p[..., :nk], vb)
             + jnp.einsum('bhjqc,bchd->bjqhd', p[..., nk:], vc))
        return o.reshape(b, GRID_W, h, dh)

    q_rows = jnp.moveaxis(ql.reshape(b, rows, GRID_W, h, dh), 1, 0)
    o = lax.map(row_fn, (jnp.arange(rows), q_rows))
    yl = jnp.moveaxis(o, 0, 1).reshape(b, s, h * dh) @ wo
    yc = None
    if need_ctx:
        sc = jnp.einsum('bqhd,bkhd->bhqk', qc, kc).astype(jnp.float32)
        pc = jax.nn.softmax(sc, axis=-1).astype(vc.dtype)
        yc = jnp.einsum('bhqk,bkhd->bqhd', pc, vc).reshape(b, uc.shape[1], h * dh) @ wo
    return yl, yc


def diff_lambda_init(layer):
    return 0.8 - 0.6 * math.exp(-0.3 * layer)


def setup_inputs(seed: int = 0) -> dict:
    key = jax.random.key(seed)
    ks = iter(jax.random.split(key, 32))

    def nrm(shape, std):
        return jax.random.normal(next(ks), shape, jnp.float32) * std

    def gain(shape):
        return 1.0 + nrm(shape, 0.01)

    d, f = D_MODEL, FFN_HIDDEN
    na, nb, nc, nd = [len(range(m, DEPTH, N_MIXERS)) for m in range(N_MIXERS)]
    swa_cols = (SWA_Q_HEADS + 2 * SWA_KV_HEADS) * HEAD_DIM
    swa_o = SWA_Q_HEADS * HEAD_DIM
    diff_o = DIFF_HEADS * 2 * HEAD_DIM
    mla_down = MLA_Q_RANK + MLA_KV_RANK + MLA_ROPE_DIM
    mla_o = MLA_HEADS * MLA_V_DIM
    na_o = NA_HEADS * HEAD_DIM
    return {
        'x': nrm((BATCH, SEQ, d), 1.0),
        'c': nrm((BATCH, d), 1.0),
        'ctx': nrm((BATCH, CTX_LEN, d), 1.0),
        'c_ctx': nrm((d,), 1.0),
        'mod_w': nrm((DEPTH, d, 6 * d), d ** -0.5),
        'mod_b': nrm((DEPTH, 6 * d), 0.01),
        'norm_g': gain((DEPTH, 4, d)),
        'ffn_w_gu': nrm((DEPTH, d, 2 * f), d ** -0.5),
        'ffn_w_down': nrm((DEPTH, f, d), f ** -0.5),
        'swa_wqkv': nrm((na, d, swa_cols), d ** -0.5),
        'swa_wo': nrm((na, swa_o, d), swa_o ** -0.5),
        'swa_sink': nrm((na, SWA_Q_HEADS), 0.5),
        'diff_wqkv': nrm((nb, d, 3 * diff_o), d ** -0.5),
        'diff_wo': nrm((nb, diff_o, d), diff_o ** -0.5),
        'diff_lambda': nrm((nb, 4, HEAD_DIM), 0.1),
        'diff_subln': gain((nb, 2 * HEAD_DIM)),
        'mla_wdown': nrm((nc, d, mla_down), d ** -0.5),
        'mla_q_norm': gain((nc, MLA_Q_RANK)),
        'mla_kv_norm': gain((nc, MLA_KV_RANK)),
        'mla_wuq': nrm((nc, MLA_Q_RANK, MLA_HEADS * (MLA_NOPE_DIM + MLA_ROPE_DIM)), MLA_Q_RANK ** -0.5),
        'mla_wukv': nrm((nc, MLA_KV_RANK, MLA_HEADS * (MLA_NOPE_DIM + MLA_V_DIM)), MLA_KV_RANK ** -0.5),
        'mla_wo': nrm((nc, mla_o, d), mla_o ** -0.5),
        'na_wqkv': nrm((nd, d, 3 * na_o), d ** -0.5),
        'na_wo': nrm((nd, na_o, d), na_o ** -0.5),
        'na_rpb': nrm((nd, NA_HEADS, 2 * NA_WIN_ROWS - 1, 2 * NA_WIN_COLS - 1), 0.1),
    }


def reference(x, c, ctx, c_ctx, mod_w, mod_b, norm_g, ffn_w_gu, ffn_w_down,
              swa_wqkv, swa_wo, swa_sink,
              diff_wqkv, diff_wo, diff_lambda, diff_subln,
              mla_wdown, mla_q_norm, mla_kv_norm, mla_wuq, mla_wukv, mla_wo,
              na_wqkv, na_wo, na_rpb):
    s = x.shape[1]
    ang_head = axial_angles(s, HEAD_DIM)
    ang_mla = axial_angles(s, MLA_ROPE_DIM)
    hl, hc = x, ctx
    for i in range(DEPTH):
        kind, j = i % N_MIXERS, i // N_MIXERS
        need_ctx = i < DEPTH - 1
        ml = [m[:, None, :] for m in adaln(c, mod_w[i], mod_b[i])]
        mc = adaln(c_ctx, mod_w[i], mod_b[i])
        g_pre_mix, g_post_mix, g_pre_ffn, g_post_ffn = norm_g[i]

        ul = rmsnorm(hl, g_pre_mix) * (1.0 + ml[1]) + ml[0]
        uc = rmsnorm(hc, g_pre_mix) * (1.0 + mc[1]) + mc[0]
        if kind == 0:
            yl, yc = swa_mixer(ul, uc, swa_wqkv[j], swa_wo[j], swa_sink[j], ang_head, need_ctx)
        elif kind == 1:
            yl, yc = diff_mixer(ul, uc, diff_wqkv[j], diff_wo[j], diff_lambda[j], diff_subln[j],
                                diff_lambda_init(i), ang_head, need_ctx)
        elif kind == 2:
            yl, yc = mla_mixer(ul, uc, mla_wdown[j], mla_q_norm[j], mla_kv_norm[j], mla_wuq[j],
                               mla_wukv[j], mla_wo[j], ang_mla, need_ctx)
        else:
            yl, yc = na_mixer(ul, uc, na_wqkv[j], na_wo[j], na_rpb[j], need_ctx)

        hl = hl + ml[2] * rmsnorm(yl, g_post_mix)
        fl = swiglu(rmsnorm(hl, g_pre_ffn) * (1.0 + ml[4]) + ml[3], ffn_w_gu[i], ffn_w_down[i])
        hl = hl + ml[5] * rmsnorm(fl, g_post_ffn)
        if need_ctx:
            hc = hc + mc[2] * rmsnorm(yc, g_post_mix)
            fc = swiglu(rmsnorm(hc, g_pre_ffn) * (1.0 + mc[4]) + mc[3], ffn_w_gu[i], ffn_w_down[i])
            hc = hc + mc[5] * rmsnorm(fc, g_post_ffn)
    return hl
```

```python
import functools
import math

import numpy as np
import jax
import jax.numpy as jnp
from jax import lax
from jax.experimental import pallas as pl
from jax.experimental.pallas import tpu as pltpu

D_MODEL = 2048
BATCH = 4
SEQ = 4096
DEPTH = 4
GRID_W = 64
CTX_LEN = 256
N_MIXERS = 4
HEAD_DIM = 128
ROPE_THETA = 10000.0
EPS = 1e-6
NEG_INF = -1e30
FFN_HIDDEN = 5632

SWA_Q_HEADS = 16
SWA_KV_HEADS = 4
SWA_GROUP = SWA_Q_HEADS // SWA_KV_HEADS
SWA_WINDOW = 128
DIFF_HEADS = 8
MLA_HEADS = 16
MLA_Q_RANK = 512
MLA_KV_RANK = 512
MLA_NOPE_DIM = 128
MLA_ROPE_DIM = 64
MLA_V_DIM = 128
NA_HEADS = 16
NA_WIN_ROWS = 8
NA_WIN_COLS = 16

N_LAT = BATCH * SEQ
N_CTX = BATCH * CTX_LEN
M_ROWS = N_LAT + N_CTX
N_COND = 8
LANES = 128

TQ = 256
NQ = SEQ // TQ
TM = 1024
TM_EW = 512
VMEM_LIMIT = 52 * 1024 * 1024

F32 = jnp.float32
BF16 = jnp.bfloat16


def _params(sem, vmem=VMEM_LIMIT):
    return pltpu.CompilerParams(dimension_semantics=sem, vmem_limit_bytes=vmem)


def _dot(a, b):
    return jnp.dot(a, b, preferred_element_type=F32)


def _dot_nt(a, b):
    return lax.dot_general(a, b, (((1,), (1,)), ((), ())), preferred_element_type=F32)


def _mod_group(i, tm):
    return (i * tm) // SEQ


def _adaln_kernel(cond_ref, w_ref, b_ref, o_ref):
    x = cond_ref[...]
    a = (x * jax.nn.sigmoid(x)).astype(BF16)
    o_ref[0] = _dot(a, w_ref[0].astype(BF16)) + b_ref[0]


def adaln_all(cond, mod_w, mod_b):
    tn = 1024
    n = 6 * D_MODEL
    out = pl.pallas_call(
        _adaln_kernel,
        grid=(DEPTH, n // tn),
        in_specs=[
            pl.BlockSpec((N_COND, D_MODEL), lambda l, j: (0, 0)),
            pl.BlockSpec((1, D_MODEL, tn), lambda l, j: (l, 0, j)),
            pl.BlockSpec((1, 1, tn), lambda l, j: (l, 0, j)),
        ],
        out_specs=pl.BlockSpec((1, N_COND, tn), lambda l, j: (l, 0, j)),
        out_shape=jax.ShapeDtypeStruct((DEPTH, N_COND, n), F32),
        compiler_params=_params(("arbitrary", "arbitrary")),
        name="adaln",
    )(cond, mod_w, mod_b.reshape(DEPTH, 1, n))
    return out.reshape(DEPTH, N_COND, 6, D_MODEL)


def _norm_mod_kernel(x_ref, g_ref, mod_ref, o_ref, *, k_shift, k_scale):
    x = x_ref[...]
    ms = jnp.mean(x * x, axis=-1, keepdims=True)
    y = x * lax.rsqrt(ms + EPS) * g_ref[...]
    shift = mod_ref[0, k_shift:k_shift + 1, :]
    scale = mod_ref[0, k_scale:k_scale + 1, :]
    o_ref[...] = (y * (1.0 + scale) + shift).astype(o_ref.dtype)


def norm_mod(h, g, mods, k_shift, k_scale):
    tm = TM_EW
    rows = h.shape[0]
    return pl.pallas_call(
        functools.partial(_norm_mod_kernel, k_shift=k_shift, k_scale=k_scale),
        grid=(rows // tm,),
        in_specs=[
            pl.BlockSpec((tm, D_MODEL), lambda i: (i, 0)),
            pl.BlockSpec((1, D_MODEL), lambda i: (0, 0)),
            pl.BlockSpec((1, 6, D_MODEL), lambda i: (_mod_group(i, tm), 0, 0)),
        ],
        out_specs=pl.BlockSpec((tm, D_MODEL), lambda i: (i, 0)),
        out_shape=jax.ShapeDtypeStruct((rows, D_MODEL), BF16),
        compiler_params=_params(("arbitrary",)),
        name="norm_mod",
    )(h, g.reshape(1, D_MODEL), mods)


def _resid_kernel(h_ref, y_ref, g_ref, mod_ref, o_ref, *, k_gate):
    y = y_ref[...]
    ms = jnp.mean(y * y, axis=-1, keepdims=True)
    yn = y * lax.rsqrt(ms + EPS) * g_ref[...]
    gate = mod_ref[0, k_gate:k_gate + 1, :]
    o_ref[...] = h_ref[...] + gate * yn


def resid(h, y, g, mods, k_gate):
    tm = TM_EW
    rows = y.shape[0]
    return pl.pallas_call(
        functools.partial(_resid_kernel, k_gate=k_gate),
        grid=(rows // tm,),
        in_specs=[
            pl.BlockSpec((tm, D_MODEL), lambda i: (i, 0)),
            pl.BlockSpec((tm, D_MODEL), lambda i: (i, 0)),
            pl.BlockSpec((1, D_MODEL), lambda i: (0, 0)),
            pl.BlockSpec((1, 6, D_MODEL), lambda i: (_mod_group(i, tm), 0, 0)),
        ],
        out_specs=pl.BlockSpec((tm, D_MODEL), lambda i: (i, 0)),
        out_shape=jax.ShapeDtypeStruct((rows, D_MODEL), F32),
        compiler_params=_params(("arbitrary",)),
        name="resid",
    )(h, y, g.reshape(1, D_MODEL), mods)


def _group_norm_kernel(x_ref, g_ref, o_ref):
    x = x_ref[...]
    ms = jnp.mean(x * x, axis=-1, keepdims=True)
    o_ref[...] = (x * lax.rsqrt(ms + EPS) * g_ref[...]).astype(o_ref.dtype)


def group_norm(x, g, width):
    tm = TM_EW
    m, n = x.shape
    return pl.pallas_call(
        _group_norm_kernel,
        grid=(m // tm, n // width),
        in_specs=[
            pl.BlockSpec((tm, width), lambda i, j: (i, j)),
            pl.BlockSpec((1, width), lambda i, j: (0, j)),
        ],
        out_specs=pl.BlockSpec((tm, width), lambda i, j: (i, j)),
        out_shape=jax.ShapeDtypeStruct((m, n), BF16),
        compiler_params=_params(("arbitrary", "arbitrary")),
        name="group_norm",
    )(x, g.reshape(1, n))


def rope_tables(rot_dim, scale, tm):
    n = rot_dim // 4
    t = jnp.arange(SEQ)
    row = (t // GRID_W).astype(F32)
    col = (t % GRID_W).astype(F32)
    inv = ROPE_THETA ** (-jnp.arange(n, dtype=F32) / n)
    ang_r = row[:, None] * inv
    ang_c = col[:, None] * inv
    cos = jnp.concatenate([jnp.cos(ang_r)] * 2 + [jnp.cos(ang_c)] * 2, axis=-1)
    sin_r, sin_c = jnp.sin(ang_r), jnp.sin(ang_c)
    zero = jnp.zeros_like(sin_r)
    sa = jnp.concatenate([-sin_r, zero, -sin_c, zero], axis=-1)
    sb = jnp.concatenate([zero, sin_r, zero, sin_c], axis=-1)
    pad = LANES - rot_dim
    c = jnp.pad(cos, ((0, tm), (0, pad)), constant_values=1.0)
    sa = jnp.pad(sa, ((0, tm), (0, pad)))
    sb = jnp.pad(sb, ((0, tm), (0, pad)))
    return jnp.stack([c, sa, sb]) * scale


def _rope_block(i, tm):
    return jnp.where(i < N_LAT // tm, i % (SEQ // tm), SEQ // tm)


def _proj_kernel(*refs, modes, shift):
    if "rope" in modes:
        x_ref, w_ref, tab_ref, o_ref = refs
    else:
        x_ref, w_ref, o_ref = refs
    acc = _dot(x_ref[...], w_ref[...])
    for s, mode in enumerate(modes):
        slab = acc[:, s * LANES:(s + 1) * LANES]
        if mode == "rope":
            slab = (slab * tab_ref[0]
                    + pltpu.roll(slab, LANES - shift, 1) * tab_ref[1]
                    + pltpu.roll(slab, shift, 1) * tab_ref[2])
        elif mode != "cast":
            slab = slab * mode
        o_ref[:, s * LANES:(s + 1) * LANES] = slab.astype(o_ref.dtype)


def proj(x, w, *, modes=("cast",), tables=None, shift=0, xcol=0, out_dtype=BF16, tn=512, rows=None):
    k, n = w.shape
    tm = TM
    rows = x.shape[0] if rows is None else rows
    tn = min(tn, n)
    per_tile = tn // LANES
    tile_modes = tuple(modes[s % len(modes)] for s in range(per_tile))
    assert per_tile % len(modes) == 0 and n % tn == 0
    in_specs = [
        pl.BlockSpec((tm, k), lambda i, j: (i, xcol)),
        pl.BlockSpec((k, tn), lambda i, j: (0, j)),
    ]
    args = [x, w]
    if "rope" in tile_modes:
        in_specs.append(pl.BlockSpec((3, tm, LANES), lambda i, j: (0, _rope_block(i, tm), 0)))
        args.append(tables)
    return pl.pallas_call(
        functools.partial(_proj_kernel, modes=tile_modes, shift=shift),
        grid=(rows // tm, n // tn),
        in_specs=in_specs,
        out_specs=pl.BlockSpec((tm, tn), lambda i, j: (i, j)),
        out_shape=jax.ShapeDtypeStruct((rows, n), out_dtype),
        compiler_params=_params(("arbitrary", "arbitrary")),
        name="proj",
    )(*args)


def _ffn_kernel(x_ref, wg_ref, wu_ref, wd_ref, o_ref):
    f = pl.program_id(1)
    x = x_ref[...]
    g = _dot(x, wg_ref[...])
    u = _dot(x, wu_ref[...])
    hdn = (g * jax.nn.sigmoid(g) * u).astype(BF16)
    part = _dot(hdn, wd_ref[...])

    @pl.when(f == 0)
    def _():
        o_ref[...] = part

    @pl.when(f > 0)
    def _():
        o_ref[...] += part


def ffn(u, w_gu, w_down):
    tm, tf = TM, 512
    nf = FFN_HIDDEN // tf
    rows = u.shape[0]
    return pl.pallas_call(
        _ffn_kernel,
        grid=(rows // tm, nf),
        in_specs=[
            pl.BlockSpec((tm, D_MODEL), lambda i, f: (i, 0)),
            pl.BlockSpec((D_MODEL, tf), lambda i, f: (0, f)),
            pl.BlockSpec((D_MODEL, tf), lambda i, f: (0, nf + f)),
            pl.BlockSpec((tf, D_MODEL), lambda i, f: (f, 0)),
        ],
        out_specs=pl.BlockSpec((tm, D_MODEL), lambda i, f: (i, 0)),
        out_shape=jax.ShapeDtypeStruct((rows, D_MODEL), F32),
        compiler_params=_params(("arbitrary", "arbitrary")),
        name="ffn",
    )(u, w_gu, w_gu, w_down)


def _q_rows(b, qi):
    return jnp.where(qi < NQ, b * NQ + qi, N_LAT // TQ + b)


def _lat_or_ctx(body, has_ctx):
    if not has_ctx:
        body(True)
        return
    qi = pl.program_id(2)

    @pl.when(qi < NQ)
    def _():
        body(True)

    @pl.when(qi == NQ)
    def _():
        body(False)


def _attn_call(kernel, *, heads, qw, kw, vw, ow, has_ctx, q, k, v, extra_specs=(), extra_args=(),
               k_extra=None, name):
    nsteps = NQ + (1 if has_ctx else 0)
    ctx0 = N_LAT // CTX_LEN
    in_specs = list(extra_specs) + [pl.BlockSpec((TQ, qw), lambda b, h, qi: (_q_rows(b, qi), h))]
    args = list(extra_args) + [q]
    for rows, row0 in ((SEQ, 0), (CTX_LEN, ctx0)):
        in_specs.append(pl.BlockSpec((rows, kw), lambda b, h, qi, row0=row0: (row0 + b, h)))
        args.append(k)
        if v is not None:
            in_specs.append(pl.BlockSpec((rows, vw), lambda b, h, qi, row0=row0: (row0 + b, h)))
            args.append(v)
    if k_extra is not None:
        w = k_extra.shape[1]
        in_specs += [
            pl.BlockSpec((SEQ, w), lambda b, h, qi: (b, 0)),
            pl.BlockSpec((CTX_LEN, w), lambda b, h, qi: (ctx0 + b, 0)),
        ]
        args += [k_extra, k_extra]
    return pl.pallas_call(
        kernel,
        grid=(BATCH, heads, nsteps),
        in_specs=in_specs,
        out_specs=pl.BlockSpec((TQ, ow), lambda b, h, qi: (_q_rows(b, qi), h)),
        out_shape=jax.ShapeDtypeStruct((M_ROWS, heads * ow), BF16),
        compiler_params=_params(("arbitrary", "arbitrary", "arbitrary")),
        name=name,
    )(*args)


SWA_KEYS = TQ + 2 * SWA_WINDOW


def _swa_kernel(sink_ref, q_ref, kl_ref, vl_ref, kc_ref, vc_ref, o_ref, *, has_ctx):
    kvh = pl.program_id(1)
    qi = pl.program_id(2)

    def body(with_lat):
        if with_lat:
            q0 = qi * TQ
            start = pl.multiple_of(jnp.clip(q0 - SWA_WINDOW, 0, SEQ - SWA_KEYS), SWA_WINDOW)
            kwin = kl_ref[pl.ds(start, SWA_KEYS), :]
            vwin = vl_ref[pl.ds(start, SWA_KEYS), :]
            qpos = q0 + lax.broadcasted_iota(jnp.int32, (TQ, SWA_KEYS), 0)
            kpos = start + lax.broadcasted_iota(jnp.int32, (TQ, SWA_KEYS), 1)
            valid = jnp.abs(kpos - qpos) <= SWA_WINDOW
        kc = kc_ref[...]
        vc = vc_ref[...]
        for g in range(SWA_GROUP):
            cols = slice(g * HEAD_DIM, (g + 1) * HEAD_DIM)
            q = q_ref[:, cols]
            sink = sink_ref[kvh * SWA_GROUP + g]
            sc = _dot_nt(q, kc)
            m = jnp.maximum(jnp.max(sc, axis=-1, keepdims=True), sink)
            if with_lat:
                sb = jnp.where(valid, _dot_nt(q, kwin), NEG_INF)
                m = jnp.maximum(m, jnp.max(sb, axis=-1, keepdims=True))
            pc = jnp.exp(sc - m)
            l = jnp.sum(pc, axis=-1, keepdims=True) + jnp.exp(sink - m)
            o = _dot(pc.astype(BF16), vc)
            if with_lat:
                pb = jnp.exp(sb - m)
                l = l + jnp.sum(pb, axis=-1, keepdims=True)
                o = o + _dot(pb.astype(BF16), vwin)
            o_ref[:, cols] = (o * (1.0 / l)).astype(o_ref.dtype)

    _lat_or_ctx(body, has_ctx)


def swa_attention(q, k, v, sink, has_ctx):
    w = SWA_GROUP * HEAD_DIM
    return _attn_call(
        functools.partial(_swa_kernel, has_ctx=has_ctx),
        heads=SWA_KV_HEADS, qw=w, kw=HEAD_DIM, vw=HEAD_DIM, ow=w, has_ctx=has_ctx,
        q=q, k=k, v=v,
        extra_specs=[pl.BlockSpec(memory_space=pltpu.SMEM)], extra_args=[sink],
        name="swa_attn")


def _diff_kernel(lam_ref, sub_ref, q_ref, kl_ref, vl_ref, kc_ref, vc_ref, o_ref, *, lam_init, has_ctx):
    lv = lam_ref[...]
    lam = (jnp.exp(jnp.sum(lv[0:1] * lv[1:2], axis=-1, keepdims=True))
           - jnp.exp(jnp.sum(lv[2:3] * lv[3:4], axis=-1, keepdims=True)) + lam_init)

    def body(with_lat):
        parts = []
        for c in range(2):
            cols = slice(c * HEAD_DIM, (c + 1) * HEAD_DIM)
            q = q_ref[:, cols]
            sc = _dot_nt(q, kc_ref[:, cols])
            m = jnp.max(sc, axis=-1, keepdims=True)
            if with_lat:
                sl = _dot_nt(q, kl_ref[:, cols])
                m = jnp.maximum(m, jnp.max(sl, axis=-1, keepdims=True))
            pc = jnp.exp(sc - m)
            l = jnp.sum(pc, axis=-1, keepdims=True)
            pl_ = None
            if with_lat:
                pl_ = jnp.exp(sl - m)
                l = l + jnp.sum(pl_, axis=-1, keepdims=True)
            parts.append((pc, pl_, 1.0 / l))
        (pc1, pl1, r1), (pc2, pl2, r2) = parts
        w2 = lam * r2
        o = _dot((pc1 * r1 - pc2 * w2).astype(BF16), vc_ref[...])
        if with_lat:
            o = o + _dot((pl1 * r1 - pl2 * w2).astype(BF16), vl_ref[...])
        ms = jnp.mean(o * o, axis=-1, keepdims=True)
        o = (o * lax.rsqrt(ms + EPS) * sub_ref[...]) * (1.0 - lam_init)
        o_ref[...] = o.astype(o_ref.dtype)

    _lat_or_ctx(body, has_ctx)


def diff_attention(q, k, v, lam_vecs, subln, lam_init, has_ctx):
    w = 2 * HEAD_DIM
    return _attn_call(
        functools.partial(_diff_kernel, lam_init=lam_init, has_ctx=has_ctx),
        heads=DIFF_HEADS, qw=w, kw=w, vw=w, ow=w, has_ctx=has_ctx,
        q=q, k=k, v=v,
        extra_specs=[pl.BlockSpec((4, HEAD_DIM), lambda b, h, qi: (0, 0)),
                     pl.BlockSpec((1, w), lambda b, h, qi: (0, 0))],
        extra_args=[lam_vecs, subln.reshape(1, w)],
        name="diff_attn")


def _mla_kernel(q_ref, kvl_ref, kvc_ref, kpl_ref, kpc_ref, o_ref, *, has_ctx):
    def body(with_lat):
        qn = q_ref[:, :MLA_NOPE_DIM]
        qp = q_ref[:, MLA_NOPE_DIM:]
        sc = _dot_nt(qn, kvc_ref[:, :MLA_NOPE_DIM]) + _dot_nt(qp, kpc_ref[...])
        m = jnp.max(sc, axis=-1, keepdims=True)
        if with_lat:
            sl = _dot_nt(qn, kvl_ref[:, :MLA_NOPE_DIM]) + _dot_nt(qp, kpl_ref[...])
            m = jnp.maximum(m, jnp.max(sl, axis=-1, keepdims=True))
        pc = jnp.exp(sc - m)
        l = jnp.sum(pc, axis=-1, keepdims=True)
        o = _dot(pc.astype(BF16), kvc_ref[:, MLA_NOPE_DIM:])
        if with_lat:
            pl_ = jnp.exp(sl - m)
            l = l + jnp.sum(pl_, axis=-1, keepdims=True)
            o = o + _dot(pl_.astype(BF16), kvl_ref[:, MLA_NOPE_DIM:])
        o_ref[...] = (o * (1.0 / l)).astype(o_ref.dtype)

    _lat_or_ctx(body, has_ctx)


def mla_attention(q, kv, kp, has_ctx):
    w = MLA_NOPE_DIM + MLA_V_DIM
    return _attn_call(
        functools.partial(_mla_kernel, has_ctx=has_ctx),
        heads=MLA_HEADS, qw=w, kw=w, vw=w, ow=MLA_V_DIM, has_ctx=has_ctx,
        q=q, k=kv, v=None, k_extra=kp, name="mla_attn")


NA_TILE_ROWS = TQ // GRID_W
NA_KEY_ROWS = NA_TILE_ROWS + NA_WIN_ROWS
NA_KEYS = NA_KEY_ROWS * GRID_W


def _na_key_start_row(t):
    return np.clip(t * NA_TILE_ROWS - NA_WIN_ROWS // 2, 0, GRID_W - NA_KEY_ROWS)


def na_bias_table(rpb):
    rows = SEQ // GRID_W
    tiles = [0, 1, NQ - 1]
    qr = np.arange(NA_TILE_ROWS)[:, None, None, None]
    qc = np.arange(GRID_W)[None, :, None, None]
    kr = np.arange(NA_KEY_ROWS)[None, None, :, None]
    kc = np.arange(GRID_W)[None, None, None, :]
    ridx, cidx, ok = [], [], []
    for t in tiles:
        r = t * NA_TILE_ROWS + qr
        kra = _na_key_start_row(t) + kr
        r0 = np.clip(r - NA_WIN_ROWS // 2, 0, rows - NA_WIN_ROWS)
        c0 = np.clip(qc - NA_WIN_COLS // 2, 0, GRID_W - NA_WIN_COLS)
        good = (kra >= r0) & (kra < r0 + NA_WIN_ROWS) & (kc >= c0) & (kc < c0 + NA_WIN_COLS)
        shape = (TQ, NA_KEYS)
        ok.append(np.broadcast_to(good, (NA_TILE_ROWS, GRID_W, NA_KEY_ROWS, GRID_W)).reshape(shape))
        ri = np.clip(kra - r + NA_WIN_ROWS - 1, 0, 2 * NA_WIN_ROWS - 2) + 0 * qc + 0 * kc
        ci = np.clip(kc - qc + NA_WIN_COLS - 1, 0, 2 * NA_WIN_COLS - 2) + 0 * qr + 0 * kr
        ridx.append(ri.reshape(shape))
        cidx.append(ci.reshape(shape))
    ridx, cidx, ok = np.stack(ridx), np.stack(cidx), np.stack(ok)
    return jnp.where(ok, rpb.astype(F32)[:, ridx, cidx], NEG_INF)


def _na_kernel(bias_ref, q_ref, kl_ref, vl_ref, kc_ref, vc_ref, o_ref, *, has_ctx):
    qi = pl.program_id(2)

    def body(with_lat):
        q = q_ref[...]
        sc = _dot_nt(q, kc_ref[...])
        m = jnp.max(sc, axis=-1, keepdims=True)
        if with_lat:
            row0 = jnp.clip(qi * NA_TILE_ROWS - NA_WIN_ROWS // 2, 0, GRID_W - NA_KEY_ROWS)
            start = pl.multiple_of(row0 * GRID_W, TQ)
            sn = _dot_nt(q, kl_ref[pl.ds(start, NA_KEYS), :]) + bias_ref[0, 0]
            m = jnp.maximum(m, jnp.max(sn, axis=-1, keepdims=True))
        pc = jnp.exp(sc - m)
        l = jnp.sum(pc, axis=-1, keepdims=True)
        o = _dot(pc.astype(BF16), vc_ref[...])
        if with_lat:
            pn = jnp.exp(sn - m)
            l = l + jnp.sum(pn, axis=-1, keepdims=True)
            o = o + _dot(pn.astype(BF16), vl_ref[pl.ds(start, NA_KEYS), :])
        o_ref[...] = (o * (1.0 / l)).astype(o_ref.dtype)

    _lat_or_ctx(body, has_ctx)


def na_attention(q, k, v, bias, has_ctx):
    def variant(b, h, qi):
        return (h, jnp.where(qi == 0, 0, jnp.where(qi >= NQ - 1, 2, 1)), 0, 0)

    return _attn_call(
        functools.partial(_na_kernel, has_ctx=has_ctx),
        heads=NA_HEADS, qw=HEAD_DIM, kw=HEAD_DIM, vw=HEAD_DIM, ow=HEAD_DIM, has_ctx=has_ctx,
        q=q, k=k, v=v,
        extra_specs=[pl.BlockSpec((1, 1, TQ, NA_KEYS), variant)], extra_args=[bias],
        name="na_attn")


def swa_mixer(u, wqkv, wo, sink, tab_q, tab_k, has_ctx):
    nq, nk = SWA_Q_HEADS * HEAD_DIM, SWA_KV_HEADS * HEAD_DIM
    w = wqkv.astype(BF16)
    q = proj(u, w[:, :nq], modes=("rope",), tables=tab_q, shift=HEAD_DIM // 4)
    k = proj(u, w[:, nq:nq + nk], modes=("rope",), tables=tab_k, shift=HEAD_DIM // 4)
    v = proj(u, w[:, nq + nk:])
    o = swa_attention(q, k, v, sink.astype(F32), has_ctx)
    return proj(o, wo.astype(BF16), out_dtype=F32, rows=M_ROWS if has_ctx else N_LAT)


def diff_mixer(u, wqkv, wo, lam_vecs, subln, lam_init, tab_q, tab_k, has_ctx):
    n = DIFF_HEADS * 2 * HEAD_DIM
    w = wqkv.astype(BF16)
    q = proj(u, w[:, :n], modes=("rope",), tables=tab_q, shift=HEAD_DIM // 4)
    k = proj(u, w[:, n:2 * n], modes=("rope",), tables=tab_k, shift=HEAD_DIM // 4)
    v = proj(u, w[:, 2 * n:])
    o = diff_attention(q, k, v, lam_vecs.astype(F32), subln.astype(F32), lam_init, has_ctx)
    return proj(o, wo.astype(BF16), out_dtype=F32, rows=M_ROWS if has_ctx else N_LAT)


def mla_mixer(u, wdown, q_norm, kv_norm, wuq, wukv, wo, tab_q, tab_k, has_ctx):
    h, dn, dr = MLA_HEADS, MLA_NOPE_DIM, MLA_ROPE_DIM
    scale = (dn + dr) ** -0.5
    nc = MLA_Q_RANK + MLA_KV_RANK
    wd = wdown.astype(BF16)
    c = proj(u, wd[:, :nc], out_dtype=F32)
    wkpe = jnp.pad(wd[:, nc:], ((0, 0), (0, LANES - dr)))
    kp = proj(u, wkpe, modes=("rope",), tables=tab_k, shift=dr // 4)
    cn = group_norm(c, jnp.concatenate([q_norm, kv_norm]).astype(F32), MLA_Q_RANK)
    wq = jnp.pad(wuq.astype(BF16).reshape(MLA_Q_RANK, h, dn + dr), ((0, 0), (0, 0), (0, 2 * LANES - dn - dr)))
    q = proj(cn, wq.reshape(MLA_Q_RANK, h * 2 * LANES), modes=(scale, "rope"), tables=tab_q, shift=dr // 4)
    kv = proj(cn, wukv.astype(BF16), xcol=1)
    o = mla_attention(q, kv, kp, has_ctx)
    return proj(o, wo.astype(BF16), out_dtype=F32, rows=M_ROWS if has_ctx else N_LAT)


def na_mixer(u, wqkv, wo, rpb, has_ctx):
    n = NA_HEADS * HEAD_DIM
    w = wqkv.astype(BF16)
    q = proj(u, w[:, :n], modes=(HEAD_DIM ** -0.5,))
    k = proj(u, w[:, n:2 * n])
    v = proj(u, w[:, 2 * n:])
    o = na_attention(q, k, v, na_bias_table(rpb), has_ctx)
    return proj(o, wo.astype(BF16), out_dtype=F32, rows=M_ROWS if has_ctx else N_LAT)


def diff_lambda_init(layer):
    return 0.8 - 0.6 * math.exp(-0.3 * layer)


def kernel(x, c, ctx, c_ctx, mod_w, mod_b, norm_g, ffn_w_gu, ffn_w_down, swa_wqkv, swa_wo, swa_sink, diff_wqkv, diff_wo, diff_lambda, diff_subln, mla_wdown, mla_q_norm, mla_kv_norm, mla_wuq, mla_wukv, mla_wo, na_wqkv, na_wo, na_rpb):
    h = jnp.concatenate([x.reshape(N_LAT, D_MODEL), ctx.reshape(N_CTX, D_MODEL)], axis=0).astype(F32)
    cond = jnp.concatenate([c, c_ctx[None, :], jnp.zeros((N_COND - BATCH - 1, D_MODEL), c.dtype)], axis=0)
    mods_all = adaln_all(cond.astype(F32), mod_w, mod_b)

    head_scale = HEAD_DIM ** -0.5
    tab_head_q = rope_tables(HEAD_DIM, head_scale, TM)
    tab_head_k = rope_tables(HEAD_DIM, 1.0, TM)
    tab_mla_q = rope_tables(MLA_ROPE_DIM, (MLA_NOPE_DIM + MLA_ROPE_DIM) ** -0.5, TM)
    tab_mla_k = rope_tables(MLA_ROPE_DIM, 1.0, TM)

    for i in range(DEPTH):
        kind, j = i % N_MIXERS, i // N_MIXERS
        has_ctx = i < DEPTH - 1
        mods = mods_all[i]
        g = norm_g[i].astype(F32)
        u = norm_mod(h, g[0], mods, 0, 1)
        if kind == 0:
            y = swa_mixer(u, swa_wqkv[j], swa_wo[j], swa_sink[j], tab_head_q, tab_head_k, has_ctx)
        elif kind == 1:
            y = diff_mixer(u, diff_wqkv[j], diff_wo[j], diff_lambda[j], diff_subln[j],
                           diff_lambda_init(i), tab_head_q, tab_head_k, has_ctx)
        elif kind == 2:
            y = mla_mixer(u, mla_wdown[j], mla_q_norm[j], mla_kv_norm[j], mla_wuq[j], mla_wukv[j],
                          mla_wo[j], tab_mla_q, tab_mla_k, has_ctx)
        else:
            y = na_mixer(u, na_wqkv[j], na_wo[j], na_rpb[j], has_ctx)
        h = resid(h, y, g[1], mods, 2)
        u = norm_mod(h, g[2], mods, 3, 4)
        f = ffn(u, ffn_w_gu[i].astype(BF16), ffn_w_down[i].astype(BF16))
        h = resid(h, f, g[3], mods, 5)
    return h[:N_LAT].reshape(BATCH, SEQ, D_MODEL)
```

```python
import functools
import math

import numpy as np
import jax
import jax.numpy as jnp
from jax import lax
from jax.experimental import pallas as pl
from jax.experimental.pallas import tpu as pltpu

D_MODEL = 2048
BATCH = 4
SEQ = 4096
DEPTH = 4
GRID_W = 64
CTX_LEN = 256
N_MIXERS = 4
HEAD_DIM = 128
ROPE_THETA = 10000.0
EPS = 1e-6
NEG_INF = -1e30
FFN_HIDDEN = 5632
LOG2E = math.log2(math.e)

SWA_Q_HEADS = 16
SWA_KV_HEADS = 4
SWA_GROUP = SWA_Q_HEADS // SWA_KV_HEADS
SWA_WINDOW = 128
DIFF_HEADS = 8
MLA_HEADS = 16
MLA_Q_RANK = 512
MLA_KV_RANK = 512
MLA_NOPE_DIM = 128
MLA_ROPE_DIM = 64
MLA_V_DIM = 128
NA_HEADS = 16
NA_WIN_ROWS = 8
NA_WIN_COLS = 16

N_LAT = BATCH * SEQ
N_CTX = BATCH * CTX_LEN
M_ROWS = N_LAT + N_CTX
N_COND = 8
LANES = 128

TQ = 256
NQ = SEQ // TQ
KCH = 1024
TM = 1024
TM_EW = 512
VMEM_LIMIT = 52 * 1024 * 1024

F32 = jnp.float32
BF16 = jnp.bfloat16


def _params(sem, vmem=VMEM_LIMIT):
    return pltpu.CompilerParams(dimension_semantics=sem, vmem_limit_bytes=vmem)


def _dot(a, b):
    return jnp.dot(a, b, preferred_element_type=F32)


def _dot_nt(a, b):
    return lax.dot_general(a, b, (((1,), (1,)), ((), ())), preferred_element_type=F32)


def _mod_group(i, tm):
    return (i * tm) // SEQ


def _rms(x):
    return x * lax.rsqrt(jnp.mean(x * x, axis=-1, keepdims=True) + EPS)


def _adaln_kernel(cond_ref, w_ref, b_ref, o_ref):
    x = cond_ref[...]
    a = (x * jax.nn.sigmoid(x)).astype(BF16)
    o_ref[0] = _dot(a, w_ref[0].astype(BF16)) + b_ref[0]


def adaln_all(cond, mod_w, mod_b):
    tn = 1024
    n = 6 * D_MODEL
    out = pl.pallas_call(
        _adaln_kernel,
        grid=(DEPTH, n // tn),
        in_specs=[
            pl.BlockSpec((N_COND, D_MODEL), lambda l, j: (0, 0)),
            pl.BlockSpec((1, D_MODEL, tn), lambda l, j: (l, 0, j)),
            pl.BlockSpec((1, 1, tn), lambda l, j: (l, 0, j)),
        ],
        out_specs=pl.BlockSpec((1, N_COND, tn), lambda l, j: (l, 0, j)),
        out_shape=jax.ShapeDtypeStruct((DEPTH, N_COND, n), F32),
        compiler_params=_params(("arbitrary", "arbitrary")),
        name="adaln",
    )(cond, mod_w, mod_b.reshape(DEPTH, 1, n))
    return out.reshape(DEPTH, N_COND, 6, D_MODEL)


def _modulate(x, g_ref, mod_ref, k_shift):
    shift = mod_ref[0, k_shift:k_shift + 1, :]
    scale = mod_ref[0, k_shift + 1:k_shift + 2, :]
    return _rms(x) * g_ref[...] * (1.0 + scale) + shift


def _norm_mod_kernel(x_ref, g_ref, mod_ref, o_ref, *, k_shift):
    o_ref[...] = _modulate(x_ref[...], g_ref, mod_ref, k_shift).astype(o_ref.dtype)


def norm_mod(h, g, mods, k_shift):
    tm = TM_EW
    rows = h.shape[0]
    return pl.pallas_call(
        functools.partial(_norm_mod_kernel, k_shift=k_shift),
        grid=(rows // tm,),
        in_specs=[
            pl.BlockSpec((tm, D_MODEL), lambda i: (i, 0)),
            pl.BlockSpec((1, D_MODEL), lambda i: (0, 0)),
            pl.BlockSpec((1, 6, D_MODEL), lambda i: (_mod_group(i, tm), 0, 0)),
        ],
        out_specs=pl.BlockSpec((tm, D_MODEL), lambda i: (i, 0)),
        out_shape=jax.ShapeDtypeStruct((rows, D_MODEL), BF16),
        compiler_params=_params(("arbitrary",)),
        name="norm_mod",
    )(h, g.reshape(1, D_MODEL), mods)


def _resid_kernel(*refs, k_gate, k_shift):
    if k_shift is None:
        h_ref, y_ref, g_ref, mod_ref, h_out = refs
    else:
        h_ref, y_ref, g_ref, mod_ref, g2_ref, mod2_ref, h_out, u_out = refs
    gate = mod_ref[0, k_gate:k_gate + 1, :]
    h = h_ref[...] + gate * (_rms(y_ref[...]) * g_ref[...])
    h_out[...] = h
    if k_shift is not None:
        u_out[...] = _modulate(h, g2_ref, mod2_ref, k_shift).astype(u_out.dtype)


def resid(h, y, g, mods, k_gate, nxt=None):
    tm = TM_EW
    rows = y.shape[0]
    row_spec = pl.BlockSpec((tm, D_MODEL), lambda i: (i, 0))
    gain_spec = pl.BlockSpec((1, D_MODEL), lambda i: (0, 0))
    mod_spec = pl.BlockSpec((1, 6, D_MODEL), lambda i: (_mod_group(i, tm), 0, 0))
    in_specs = [row_spec, row_spec, gain_spec, mod_spec]
    args = [h, y, g.reshape(1, D_MODEL), mods]
    out_specs = [row_spec]
    out_shape = [jax.ShapeDtypeStruct((rows, D_MODEL), F32)]
    k_shift = None
    if nxt is not None:
        g2, mods2, k_shift = nxt
        in_specs += [gain_spec, mod_spec]
        args += [g2.reshape(1, D_MODEL), mods2]
        out_specs.append(row_spec)
        out_shape.append(jax.ShapeDtypeStruct((rows, D_MODEL), BF16))
    out = pl.pallas_call(
        functools.partial(_resid_kernel, k_gate=k_gate, k_shift=k_shift),
        grid=(rows // tm,),
        in_specs=in_specs,
        out_specs=out_specs,
        out_shape=out_shape,
        compiler_params=_params(("arbitrary",)),
        name="resid",
    )(*args)
    return (out[0], out[1]) if nxt is not None else (out[0], None)


def _group_norm_kernel(x_ref, g_ref, o_ref):
    o_ref[...] = (_rms(x_ref[...]) * g_ref[...]).astype(o_ref.dtype)


def group_norm(x, g, width):
    tm = TM_EW
    m, n = x.shape
    return pl.pallas_call(
        _group_norm_kernel,
        grid=(m // tm, n // width),
        in_specs=[
            pl.BlockSpec((tm, width), lambda i, j: (i, j)),
            pl.BlockSpec((1, width), lambda i, j: (0, j)),
        ],
        out_specs=pl.BlockSpec((tm, width), lambda i, j: (i, j)),
        out_shape=jax.ShapeDtypeStruct((m, n), BF16),
        compiler_params=_params(("arbitrary", "arbitrary")),
        name="group_norm",
    )(x, g.reshape(1, n))


def rope_tables(rot_dim, scale, tm):
    n = rot_dim // 4
    t = jnp.arange(SEQ)
    row = (t // GRID_W).astype(F32)
    col = (t % GRID_W).astype(F32)
    inv = ROPE_THETA ** (-jnp.arange(n, dtype=F32) / n)
    ang_r = row[:, None] * inv
    ang_c = col[:, None] * inv
    cos = jnp.concatenate([jnp.cos(ang_r)] * 2 + [jnp.cos(ang_c)] * 2, axis=-1)
    sin_r, sin_c = jnp.sin(ang_r), jnp.sin(ang_c)
    zero = jnp.zeros_like(sin_r)
    sa = jnp.concatenate([-sin_r, zero, -sin_c, zero], axis=-1)
    sb = jnp.concatenate([zero, sin_r, zero, sin_c], axis=-1)
    pad = LANES - rot_dim
    c = jnp.pad(cos, ((0, tm), (0, pad)), constant_values=1.0)
    sa = jnp.pad(sa, ((0, tm), (0, pad)))
    sb = jnp.pad(sb, ((0, tm), (0, pad)))
    return jnp.stack([c, sa, sb]) * scale


def _rope_block(i, tm):
    return jnp.where(i < N_LAT // tm, i % (SEQ // tm), SEQ // tm)


def _proj_kernel(*refs, modes, shift):
    if "rope" in modes:
        x_ref, w_ref, tab_ref, o_ref = refs
    else:
        x_ref, w_ref, o_ref = refs
    acc = _dot(x_ref[...], w_ref[...])
    for s, mode in enumerate(modes):
        slab = acc[:, s * LANES:(s + 1) * LANES]
        if mode == "rope":
            slab = (slab * tab_ref[0]
                    + pltpu.roll(slab, LANES - shift, 1) * tab_ref[1]
                    + pltpu.roll(slab, shift, 1) * tab_ref[2])
        elif mode != "cast":
            slab = slab * mode
        o_ref[:, s * LANES:(s + 1) * LANES] = slab.astype(o_ref.dtype)


def proj(x, w, *, modes=("cast",), tables=None, shift=0, xcol=0, out_dtype=BF16, tn=512, rows=None):
    k, n = w.shape
    tm = TM
    rows = x.shape[0] if rows is None else rows
    tn = min(tn, n)
    per_tile = tn // LANES
    tile_modes = tuple(modes[s % len(modes)] for s in range(per_tile))
    assert per_tile % len(modes) == 0 and n % tn == 0
    in_specs = [
        pl.BlockSpec((tm, k), lambda i, j: (i, xcol)),
        pl.BlockSpec((k, tn), lambda i, j: (0, j)),
    ]
    args = [x, w]
    if "rope" in tile_modes:
        in_specs.append(pl.BlockSpec((3, tm, LANES), lambda i, j: (0, _rope_block(i, tm), 0)))
        args.append(tables)
    return pl.pallas_call(
        functools.partial(_proj_kernel, modes=tile_modes, shift=shift),
        grid=(rows // tm, n // tn),
        in_specs=in_specs,
        out_specs=pl.BlockSpec((tm, tn), lambda i, j: (i, j)),
        out_shape=jax.ShapeDtypeStruct((rows, n), out_dtype),
        compiler_params=_params(("arbitrary", "arbitrary")),
        name="proj",
    )(*args)


def _ffn_kernel(x_ref, wg_ref, wu_ref, wd_ref, o_ref):
    f = pl.program_id(1)
    x = x_ref[...]
    g = _dot(x, wg_ref[...])
    u = _dot(x, wu_ref[...])
    hdn = (g * jax.nn.sigmoid(g) * u).astype(BF16)
    part = _dot(hdn, wd_ref[...])

    @pl.when(f == 0)
    def _():
        o_ref[...] = part

    @pl.when(f > 0)
    def _():
        o_ref[...] += part


def ffn(u, w_gu, w_down):
    tm, tf = TM, 512
    nf = FFN_HIDDEN // tf
    rows = u.shape[0]
    return pl.pallas_call(
        _ffn_kernel,
        grid=(rows // tm, nf),
        in_specs=[
            pl.BlockSpec((tm, D_MODEL), lambda i, f: (i, 0)),
            pl.BlockSpec((D_MODEL, tf), lambda i, f: (0, f)),
            pl.BlockSpec((D_MODEL, tf), lambda i, f: (0, nf + f)),
            pl.BlockSpec((tf, D_MODEL), lambda i, f: (f, 0)),
        ],
        out_specs=pl.BlockSpec((tm, D_MODEL), lambda i, f: (i, 0)),
        out_shape=jax.ShapeDtypeStruct((rows, D_MODEL), F32),
        compiler_params=_params(("arbitrary", "arbitrary")),
        name="ffn",
    )(u, w_gu, w_gu, w_down)


def _q_rows(b, qi):
    return jnp.where(qi < NQ, b * NQ + qi, N_LAT // TQ + b)


def _lat_or_ctx(body, has_ctx):
    if not has_ctx:
        body(True)
        return
    qi = pl.program_id(2)

    @pl.when(qi < NQ)
    def _():
        body(True)

    @pl.when(qi == NQ)
    def _():
        body(False)


def _softmax_chain(chunks, init=None):
    m, l = init if init is not None else (None, None)
    acc = None
    for score, value in chunks:
        s = score()
        mc = jnp.max(s, axis=-1, keepdims=True)
        if m is None:
            m_new = mc
        else:
            m_new = jnp.maximum(m, mc)
            alpha = jnp.exp2(m - m_new)
            l = l * alpha
            if acc is not None:
                acc = acc * alpha
        p = jnp.exp2(s - m_new)
        ps = jnp.sum(p, axis=-1, keepdims=True)
        l = ps if l is None else l + ps
        pv = value(p.astype(BF16))
        acc = pv if acc is None else acc + pv
        m = m_new
    return acc, l


def _attn_call(kernel, *, heads, qw, kw, vw, ow, has_ctx, q, k, v, extra_specs=(), extra_args=(),
               k_extra=None, scratch_shapes=(), name):
    nsteps = NQ + (1 if has_ctx else 0)
    ctx0 = N_LAT // CTX_LEN
    in_specs = list(extra_specs) + [pl.BlockSpec((TQ, qw), lambda b, h, qi: (_q_rows(b, qi), h))]
    args = list(extra_args) + [q]
    for rows, row0 in ((SEQ, 0), (CTX_LEN, ctx0)):
        in_specs.append(pl.BlockSpec((rows, kw), lambda b, h, qi, row0=row0: (row0 + b, h)))
        args.append(k)
        if v is not None:
            in_specs.append(pl.BlockSpec((rows, vw), lambda b, h, qi, row0=row0: (row0 + b, h)))
            args.append(v)
    if k_extra is not None:
        w = k_extra.shape[1]
        in_specs += [
            pl.BlockSpec((SEQ, w), lambda b, h, qi: (b, 0)),
            pl.BlockSpec((CTX_LEN, w), lambda b, h, qi: (ctx0 + b, 0)),
        ]
        args += [k_extra, k_extra]
    return pl.pallas_call(
        kernel,
        grid=(BATCH, heads, nsteps),
        in_specs=in_specs,
        out_specs=pl.BlockSpec((TQ, ow), lambda b, h, qi: (_q_rows(b, qi), h)),
        out_shape=jax.ShapeDtypeStruct((M_ROWS, heads * ow), BF16),
        scratch_shapes=list(scratch_shapes),
        compiler_params=_params(("arbitrary", "arbitrary", "arbitrary")),
        name=name,
    )(*args)


SWA_KEYS = TQ + 2 * SWA_WINDOW


def _swa_kernel(sink_ref, q_ref, kl_ref, vl_ref, kc_ref, vc_ref, o_ref, *, has_ctx):
    kvh = pl.program_id(1)
    qi = pl.program_id(2)

    def body(with_lat):
        if with_lat:
            q0 = qi * TQ
            start = pl.multiple_of(jnp.clip(q0 - SWA_WINDOW, 0, SEQ - SWA_KEYS), SWA_WINDOW)
            kwin = kl_ref[pl.ds(start, SWA_KEYS), :]
            vwin = vl_ref[pl.ds(start, SWA_KEYS), :]
            qpos = q0 + lax.broadcasted_iota(jnp.int32, (TQ, SWA_KEYS), 0)
            kpos = start + lax.broadcasted_iota(jnp.int32, (TQ, SWA_KEYS), 1)
            valid = jnp.abs(kpos - qpos) <= SWA_WINDOW
        kc = kc_ref[...]
        vc = vc_ref[...]
        for g in range(SWA_GROUP):
            cols = slice(g * HEAD_DIM, (g + 1) * HEAD_DIM)
            q = q_ref[:, cols]
            sink = jnp.full((TQ, 1), sink_ref[kvh * SWA_GROUP + g] * LOG2E, F32)
            chunks = [(lambda q=q: _dot_nt(q, kc), lambda p: _dot(p, vc))]
            if with_lat:
                chunks.append((lambda q=q: jnp.where(valid, _dot_nt(q, kwin), NEG_INF),
                               lambda p: _dot(p, vwin)))
            o, l = _softmax_chain(chunks, init=(sink, jnp.ones((TQ, 1), F32)))
            o_ref[:, cols] = (o * (1.0 / l)).astype(o_ref.dtype)

    _lat_or_ctx(body, has_ctx)


def swa_attention(q, k, v, sink, has_ctx):
    w = SWA_GROUP * HEAD_DIM
    return _attn_call(
        functools.partial(_swa_kernel, has_ctx=has_ctx),
        heads=SWA_KV_HEADS, qw=w, kw=HEAD_DIM, vw=HEAD_DIM, ow=w, has_ctx=has_ctx,
        q=q, k=k, v=v,
        extra_specs=[pl.BlockSpec(memory_space=pltpu.SMEM)], extra_args=[sink],
        name="swa_attn")


def _diff_kernel(lam_ref, sub_ref, q_ref, kl_ref, vl_ref, kc_ref, vc_ref, o_ref, *, lam_init, has_ctx):
    lv = lam_ref[...]
    lam = (jnp.exp(jnp.sum(lv[0:1] * lv[1:2], axis=-1, keepdims=True))
           - jnp.exp(jnp.sum(lv[2:3] * lv[3:4], axis=-1, keepdims=True)) + lam_init)

    def body(with_lat):
        outs = []
        for c in range(2):
            cols = slice(c * HEAD_DIM, (c + 1) * HEAD_DIM)
            q = q_ref[:, cols]
            chunks = [(lambda q=q, cols=cols: _dot_nt(q, kc_ref[:, cols]), lambda p: _dot(p, vc_ref[...]))]
            if with_lat:
                for j in range(SEQ // KCH):
                    rows = slice(j * KCH, (j + 1) * KCH)
                    chunks.append((lambda q=q, cols=cols, rows=rows: _dot_nt(q, kl_ref[rows, cols]),
                                   lambda p, rows=rows: _dot(p, vl_ref[rows, :])))
            o, l = _softmax_chain(chunks)
            outs.append(o * (1.0 / l))
        o = outs[0] - lam * outs[1]
        o = (_rms(o) * sub_ref[...]) * (1.0 - lam_init)
        o_ref[...] = o.astype(o_ref.dtype)

    _lat_or_ctx(body, has_ctx)


def diff_attention(q, k, v, lam_vecs, subln, lam_init, has_ctx):
    w = 2 * HEAD_DIM
    return _attn_call(
        functools.partial(_diff_kernel, lam_init=lam_init, has_ctx=has_ctx),
        heads=DIFF_HEADS, qw=w, kw=w, vw=w, ow=w, has_ctx=has_ctx,
        q=q, k=k, v=v,
        extra_specs=[pl.BlockSpec((4, HEAD_DIM), lambda b, h, qi: (0, 0)),
                     pl.BlockSpec((1, w), lambda b, h, qi: (0, 0))],
        extra_args=[lam_vecs, subln.reshape(1, w)],
        name="diff_attn")


def _mla_kernel(q_ref, kvl_ref, kvc_ref, kpl_ref, kpc_ref, o_ref, kl_scr, kc_scr, *, has_ctx):
    dn = MLA_NOPE_DIM

    @pl.when(pl.program_id(2) == 0)
    def _():
        kl_scr[:, :dn] = kvl_ref[:, :dn]
        kl_scr[:, dn:] = kpl_ref[...]
        kc_scr[:, :dn] = kvc_ref[:, :dn]
        kc_scr[:, dn:] = kpc_ref[...]

    def body(with_lat):
        q = q_ref[...]
        chunks = [(lambda: _dot_nt(q, kc_scr[...]), lambda p: _dot(p, kvc_ref[:, dn:]))]
        if with_lat:
            for j in range(SEQ // KCH):
                rows = slice(j * KCH, (j + 1) * KCH)
                chunks.append((lambda rows=rows: _dot_nt(q, kl_scr[rows, :]),
                               lambda p, rows=rows: _dot(p, kvl_ref[rows, dn:])))
        o, l = _softmax_chain(chunks)
        o_ref[...] = (o * (1.0 / l)).astype(o_ref.dtype)

    _lat_or_ctx(body, has_ctx)


def mla_attention(q, kv, kp, has_ctx):
    w = MLA_NOPE_DIM + MLA_V_DIM
    return _attn_call(
        functools.partial(_mla_kernel, has_ctx=has_ctx),
        heads=MLA_HEADS, qw=w, kw=w, vw=w, ow=MLA_V_DIM, has_ctx=has_ctx,
        q=q, k=kv, v=None, k_extra=kp,
        scratch_shapes=[pltpu.VMEM((SEQ, w), BF16), pltpu.VMEM((CTX_LEN, w), BF16)],
        name="mla_attn")


NA_TILE_ROWS = TQ // GRID_W
NA_KEY_ROWS = NA_TILE_ROWS + NA_WIN_ROWS
NA_KEYS = NA_KEY_ROWS * GRID_W
NA_ROW_OFFS = 2 * NA_WIN_ROWS - 1
NA_COL_OFFS = 2 * NA_WIN_COLS - 1


def _na_key_start_row(t):
    return int(np.clip(t * NA_TILE_ROWS - NA_WIN_ROWS // 2, 0, GRID_W - NA_KEY_ROWS))


def _na_bias_kernel(base_ref, o_ref):
    rows = SEQ // GRID_W
    qc = lax.broadcasted_iota(jnp.int32, (GRID_W, LANES), 0)
    lane = lax.broadcasted_iota(jnp.int32, (GRID_W, LANES), 1)
    kc = lane % GRID_W
    c0 = jnp.clip(qc - NA_WIN_COLS // 2, 0, GRID_W - NA_WIN_COLS)
    col_ok = (kc >= c0) & (kc < c0 + NA_WIN_COLS)
    first = lane < GRID_W
    keep = {(True, True): col_ok, (True, False): col_ok & first, (False, True): col_ok & ~first}
    for v, t in enumerate((0, 1, NQ - 1)):
        for qr in range(NA_TILE_ROWS):
            r = t * NA_TILE_ROWS + qr
            r0 = int(np.clip(r - NA_WIN_ROWS // 2, 0, rows - NA_WIN_ROWS))
            for pair in range(NA_KEY_ROWS // 2):
                kra = _na_key_start_row(t) + 2 * pair
                ok = (r0 <= kra < r0 + NA_WIN_ROWS, r0 <= kra + 1 < r0 + NA_WIN_ROWS)
                if ok == (False, False):
                    slab = jnp.full((GRID_W, LANES), NEG_INF, F32)
                else:
                    d = kra - r + NA_WIN_ROWS
                    row = jnp.broadcast_to(base_ref[0, d:d + 1, :], (GRID_W, LANES))
                    rolled = pltpu.roll(row, LANES - (NA_WIN_COLS - 1), 1, stride=1, stride_axis=0)
                    slab = jnp.where(keep[ok], rolled, NEG_INF)
                o_ref[0, v, qr * GRID_W:(qr + 1) * GRID_W, pair * LANES:(pair + 1) * LANES] = slab


def na_bias_table(rpb):
    r = jnp.pad(rpb.astype(F32) * LOG2E, ((0, 0), (1, 1), (0, LANES // 2 - NA_COL_OFFS)))
    base = jnp.concatenate([r[:, :-1], r[:, 1:]], axis=-1)
    nd = NA_ROW_OFFS + 1
    return pl.pallas_call(
        _na_bias_kernel,
        grid=(NA_HEADS,),
        in_specs=[pl.BlockSpec((1, nd, LANES), lambda h: (h, 0, 0))],
        out_specs=pl.BlockSpec((1, 3, TQ, NA_KEYS), lambda h: (h, 0, 0, 0)),
        out_shape=jax.ShapeDtypeStruct((NA_HEADS, 3, TQ, NA_KEYS), F32),
        compiler_params=_params(("arbitrary",)),
        name="na_bias",
    )(base)


def _na_kernel(bias_ref, q_ref, kl_ref, vl_ref, kc_ref, vc_ref, o_ref, *, has_ctx):
    qi = pl.program_id(2)

    def body(with_lat):
        q = q_ref[...]
        chunks = [(lambda: _dot_nt(q, kc_ref[...]), lambda p: _dot(p, vc_ref[...]))]
        if with_lat:
            row0 = jnp.clip(qi * NA_TILE_ROWS - NA_WIN_ROWS // 2, 0, GRID_W - NA_KEY_ROWS)
            start = pl.multiple_of(row0 * GRID_W, TQ)
            chunks.append((lambda: _dot_nt(q, kl_ref[pl.ds(start, NA_KEYS), :]) + bias_ref[0, 0],
                           lambda p: _dot(p, vl_ref[pl.ds(start, NA_KEYS), :])))
        o, l = _softmax_chain(chunks)
        o_ref[...] = (o * (1.0 / l)).astype(o_ref.dtype)

    _lat_or_ctx(body, has_ctx)


def na_attention(q, k, v, bias, has_ctx):
    def variant(b, h, qi):
        return (h, jnp.where(qi == 0, 0, jnp.where(qi >= NQ - 1, 2, 1)), 0, 0)

    return _attn_call(
        functools.partial(_na_kernel, has_ctx=has_ctx),
        heads=NA_HEADS, qw=HEAD_DIM, kw=HEAD_DIM, vw=HEAD_DIM, ow=HEAD_DIM, has_ctx=has_ctx,
        q=q, k=k, v=v,
        extra_specs=[pl.BlockSpec((1, 1, TQ, NA_KEYS), variant)], extra_args=[bias],
        name="na_attn")


def _out_rows(has_ctx):
    return M_ROWS if has_ctx else N_LAT


def swa_mixer(u, wqkv, wo, sink, tab_q, tab_k, has_ctx):
    nq, nk = SWA_Q_HEADS * HEAD_DIM, SWA_KV_HEADS * HEAD_DIM
    w = wqkv.astype(BF16)
    q = proj(u, w[:, :nq], modes=("rope",), tables=tab_q, shift=HEAD_DIM // 4)
    k = proj(u, w[:, nq:nq + nk], modes=("rope",), tables=tab_k, shift=HEAD_DIM // 4)
    v = proj(u, w[:, nq + nk:])
    o = swa_attention(q, k, v, sink.astype(F32), has_ctx)
    return proj(o, wo.astype(BF16), out_dtype=F32, rows=_out_rows(has_ctx))


def diff_mixer(u, wqkv, wo, lam_vecs, subln, lam_init, tab_q, tab_k, has_ctx):
    n = DIFF_HEADS * 2 * HEAD_DIM
    w = wqkv.astype(BF16)
    q = proj(u, w[:, :n], modes=("rope",), tables=tab_q, shift=HEAD_DIM // 4)
    k = proj(u, w[:, n:2 * n], modes=("rope",), tables=tab_k, shift=HEAD_DIM // 4)
    v = proj(u, w[:, 2 * n:])
    o = diff_attention(q, k, v, lam_vecs.astype(F32), subln.astype(F32), lam_init, has_ctx)
    return proj(o, wo.astype(BF16), out_dtype=F32, rows=_out_rows(has_ctx))


def mla_mixer(u, wdown, q_norm, kv_norm, wuq, wukv, wo, tab_q, tab_k, has_ctx):
    h, dn, dr = MLA_HEADS, MLA_NOPE_DIM, MLA_ROPE_DIM
    scale = (dn + dr) ** -0.5 * LOG2E
    nc = MLA_Q_RANK + MLA_KV_RANK
    wd = wdown.astype(BF16)
    c = proj(u, wd[:, :nc], out_dtype=F32)
    wkpe = jnp.pad(wd[:, nc:], ((0, 0), (0, LANES - dr)))
    kp = proj(u, wkpe, modes=("rope",), tables=tab_k, shift=dr // 4)
    cn = group_norm(c, jnp.concatenate([q_norm, kv_norm]).astype(F32), MLA_Q_RANK)
    wq = jnp.pad(wuq.astype(BF16).reshape(MLA_Q_RANK, h, dn + dr), ((0, 0), (0, 0), (0, 2 * LANES - dn - dr)))
    q = proj(cn, wq.reshape(MLA_Q_RANK, h * 2 * LANES), modes=(scale, "rope"), tables=tab_q, shift=dr // 4)
    kv = proj(cn, wukv.astype(BF16), xcol=1)
    o = mla_attention(q, kv, kp, has_ctx)
    return proj(o, wo.astype(BF16), out_dtype=F32, rows=_out_rows(has_ctx))


def na_mixer(u, wqkv, wo, rpb, has_ctx):
    n = NA_HEADS * HEAD_DIM
    w = wqkv.astype(BF16)
    q = proj(u, w[:, :n], modes=(HEAD_DIM ** -0.5 * LOG2E,))
    k = proj(u, w[:, n:2 * n])
    v = proj(u, w[:, 2 * n:])
    o = na_attention(q, k, v, na_bias_table(rpb), has_ctx)
    return proj(o, wo.astype(BF16), out_dtype=F32, rows=_out_rows(has_ctx))


def diff_lambda_init(layer):
    return 0.8 - 0.6 * math.exp(-0.3 * layer)


def kernel(x, c, ctx, c_ctx, mod_w, mod_b, norm_g, ffn_w_gu, ffn_w_down, swa_wqkv, swa_wo, swa_sink, diff_wqkv, diff_wo, diff_lambda, diff_subln, mla_wdown, mla_q_norm, mla_kv_norm, mla_wuq, mla_wukv, mla_wo, na_wqkv, na_wo, na_rpb):
    h = jnp.concatenate([x.reshape(N_LAT, D_MODEL), ctx.reshape(N_CTX, D_MODEL)], axis=0).astype(F32)
    cond = jnp.concatenate([c, c_ctx[None, :], jnp.zeros((N_COND - BATCH - 1, D_MODEL), c.dtype)], axis=0)
    mods_all = adaln_all(cond.astype(F32), mod_w, mod_b)
    gains = norm_g.astype(F32)

    head_scale = HEAD_DIM ** -0.5 * LOG2E
    tab_head_q = rope_tables(HEAD_DIM, head_scale, TM)
    tab_head_k = rope_tables(HEAD_DIM, 1.0, TM)
    tab_mla_q = rope_tables(MLA_ROPE_DIM, (MLA_NOPE_DIM + MLA_ROPE_DIM) ** -0.5 * LOG2E, TM)
    tab_mla_k = rope_tables(MLA_ROPE_DIM, 1.0, TM)

    u = norm_mod(h, gains[0, 0], mods_all[0], 0)
    for i in range(DEPTH):
        kind, j = i % N_MIXERS, i // N_MIXERS
        has_ctx = i < DEPTH - 1
        mods = mods_all[i]
        g = gains[i]
        if kind == 0:
            y = swa_mixer(u, swa_wqkv[j], swa_wo[j], swa_sink[j], tab_head_q, tab_head_k, has_ctx)
        elif kind == 1:
            y = diff_mixer(u, diff_wqkv[j], diff_wo[j], diff_lambda[j], diff_subln[j],
                           diff_lambda_init(i), tab_head_q, tab_head_k, has_ctx)
        elif kind == 2:
            y = mla_mixer(u, mla_wdown[j], mla_q_norm[j], mla_kv_norm[j], mla_wuq[j], mla_wukv[j],
                          mla_wo[j], tab_mla_q, tab_mla_k, has_ctx)
        else:
            y = na_mixer(u, na_wqkv[j], na_wo[j], na_rpb[j], has_ctx)
        h, u = resid(h, y, g[1], mods, 2, nxt=(g[2], mods, 3))
        f = ffn(u, ffn_w_gu[i].astype(BF16), ffn_w_down[i].astype(BF16))
        nxt = (gains[i + 1, 0], mods_all[i + 1], 0) if has_ctx else None
        h, u = resid(h, f, g[3], mods, 5, nxt=nxt)
    return h.reshape(BATCH, SEQ, D_MODEL)
```
